```python
import jax, jax.numpy as jnp
from jax import lax
import numpy as np

D_MODEL = 2048
BATCH = 32
SEQ = 256
DEPTH = 1
DEC_BATCH = 2
DEC_SEQ = 4096
PAST_LEN = 512

GRID_W = 64
CONV_WIDTH = D_MODEL // 2
CONV_KERNEL = 31
N_RET_HEADS = 8
RET_DK = 64
RET_DV = 128
RET_QK_WIDTH = N_RET_HEADS * RET_DK
RET_V_WIDTH = N_RET_HEADS * RET_DV
RET_CHUNK = 128
FFN_HIDDEN = -(-8 * D_MODEL // (3 * 256)) * 256
ROPE_BASE = 10000.0
EPS = 1e-6
N_MOD = 6
IN_WIDTH = 2 * CONV_WIDTH + 2 * RET_QK_WIDTH + 3 * RET_V_WIDTH + 2 * D_MODEL

kernel_name = 'hybrid_conv_retention_diffusion_step'


def rms_norm(x, g):
    x32 = x.astype(jnp.float32)
    y = x32 * lax.rsqrt(jnp.mean(x32 * x32, axis=-1, keepdims=True) + EPS)
    return (y * g.astype(jnp.float32)).astype(x.dtype)


def layer_norm(x, g, b):
    x32 = x.astype(jnp.float32)
    xc = x32 - jnp.mean(x32, axis=-1, keepdims=True)
    y = xc * lax.rsqrt(jnp.mean(xc * xc, axis=-1, keepdims=True) + EPS)
    return (y * g.astype(jnp.float32) + b.astype(jnp.float32)).astype(x.dtype)


def head_rms(o):
    return o * lax.rsqrt(jnp.mean(o * o, axis=-1, keepdims=True) + EPS)


def depthwise_conv(x, w, b):
    pad = CONV_KERNEL // 2
    y = lax.conv_general_dilated(x, w[:, None, :].astype(x.dtype), window_strides=(1,),
                                 padding=[(pad, pad)], dimension_numbers=('NWC', 'WIO', 'NWC'),
                                 feature_group_count=CONV_WIDTH)
    return y + b.astype(x.dtype)


def grid_rotary(n_tokens):
    rows = n_tokens // GRID_W
    row = jnp.repeat(jnp.arange(rows, dtype=jnp.float32), GRID_W)
    col = jnp.tile(jnp.arange(GRID_W, dtype=jnp.float32), rows)
    n_freq = RET_DK // 4
    inv = ROPE_BASE ** (-jnp.arange(n_freq, dtype=jnp.float32) / n_freq)
    ang = jnp.concatenate([row[:, None] * inv, col[:, None] * inv], axis=-1)
    return jnp.cos(ang), jnp.sin(ang)


def apply_rotary(x, cos, sin):
    half = RET_DK // 2
    x1, x2 = x[..., :half], x[..., half:]
    cs = cos[None, :, None, :]
    sn = sin[None, :, None, :]
    return jnp.concatenate([x1 * cs - x2 * sn, x2 * cs + x1 * sn], axis=-1)


def retention_chunkwise(q, k, v, log_gamma, s0):
    B, T, H, _ = q.shape
    dv = v.shape[-1]
    C = RET_CHUNK
    n_chunks = T // C
    pos = jnp.arange(C, dtype=jnp.float32)
    diff = pos[:, None] - pos[None, :]
    lower = diff >= 0
    inner_decay = jnp.where(lower[None], jnp.exp(jnp.where(lower, diff, 0.0)[None] * log_gamma[:, None, None]), 0.0)
    q_decay = jnp.exp((pos + 1.0)[:, None] * log_gamma[None, :])
    k_decay = jnp.exp((C - 1.0 - pos)[:, None] * log_gamma[None, :])
    chunk_decay = jnp.exp(C * log_gamma)

    def to_chunks(a):
        return jnp.moveaxis(a.reshape(B, n_chunks, C, H, a.shape[-1]), 1, 0)

    def step(state, chunk):
        qc, kc, vc = chunk
        scores = jnp.einsum('bnhd,bmhd->bhnm', qc, kc) * inner_decay[None]
        inner = jnp.einsum('bhnm,bmhe->bnhe', scores, vc)
        cross = jnp.einsum('bnhd,bhde->bnhe', qc * q_decay[None, :, :, None], state)
        new_state = chunk_decay[None, :, None, None] * state + jnp.einsum(
            'bmhd,bmhe->bhde', kc * k_decay[None, :, :, None], vc)
        return new_state, inner + cross

    s_final, out = lax.scan(step, s0, (to_chunks(q), to_chunks(k), to_chunks(v)))
    out = jnp.moveaxis(out, 0, 1).reshape(B, T, H, dv)
    return out, s_final


def trunk_layer(x, cond, rope, s0_fwd, s0_bwd, w_ada, b_ada, norm1_g, w_in, conv_dw_w, conv_dw_b,
                conv_ln_g, conv_ln_b, w_conv_out, ret_decay_logit, w_ret_out, w_mix_out, norm2_g,
                w_ffn_in, w_ffn_out):
    f32 = jnp.float32
    B, T, _ = x.shape
    mod = jax.nn.silu(cond) @ w_ada + b_ada
    shift1, scale1, gate1, shift2, scale2, gate2 = jnp.split(mod[:, None, :], N_MOD, axis=-1)

    h = rms_norm(x, norm1_g) * (1.0 + scale1) + shift1
    proj = h @ w_in
    widths = [2 * CONV_WIDTH, RET_QK_WIDTH, RET_QK_WIDTH, RET_V_WIDTH, RET_V_WIDTH, RET_V_WIDTH, D_MODEL, D_MODEL]
    splits = np.cumsum(widths)[:-1].tolist()
    conv_u, q, k, v, g_fwd, g_bwd, gate_a, gate_b = jnp.split(proj, splits, axis=-1)

    a = conv_u[..., :CONV_WIDTH] * jax.nn.sigmoid(conv_u[..., CONV_WIDTH:])
    a = depthwise_conv(a, conv_dw_w, conv_dw_b)
    a = jax.nn.silu(layer_norm(a, conv_ln_g, conv_ln_b))
    branch_a = a @ w_conv_out

    q = q.astype(f32).reshape(B, T, N_RET_HEADS, RET_DK)
    k = k.astype(f32).reshape(B, T, N_RET_HEADS, RET_DK) * (RET_DK ** -0.5)
    v = v.astype(f32).reshape(B, T, N_RET_HEADS, RET_DV)
    if rope is not None:
        q = apply_rotary(q, rope[0], rope[1])
        k = apply_rotary(k, rope[0], rope[1])
    log_gamma = jax.nn.log_sigmoid(ret_decay_logit.astype(f32))
    o_f, s_f = retention_chunkwise(q, k, v, log_gamma[0], s0_fwd.astype(f32))
    o_b, s_b = retention_chunkwise(q[:, ::-1], k[:, ::-1], v[:, ::-1], log_gamma[1], s0_bwd.astype(f32))
    o_b = o_b[:, ::-1]
    ret = (jax.nn.silu(g_fwd.astype(f32)) * head_rms(o_f).reshape(B, T, RET_V_WIDTH)
           + jax.nn.silu(g_bwd.astype(f32)) * head_rms(o_b).reshape(B, T, RET_V_WIDTH)).astype(x.dtype)
    branch_b = ret @ w_ret_out

    merged = jax.nn.sigmoid(gate_a) * branch_a + jax.nn.sigmoid(gate_b) * branch_b
    x = x + gate1 * (merged @ w_mix_out)

    h2 = rms_norm(x, norm2_g) * (1.0 + scale2) + shift2
    f_gate, f_up = jnp.split(h2 @ w_ffn_in, 2, axis=-1)
    x = x + gate2 * ((jax.nn.silu(f_gate) * f_up) @ w_ffn_out)
    return x, s_f, s_b


def setup_inputs(seed: int = 0) -> dict:
    key = jax.random.key(seed)
    ks = jax.random.split(key, 24)
    f32 = jnp.float32
    D = D_MODEL

    def nrm(k, shape, scale):
        return jax.random.normal(k, shape, f32) * scale

    base_logit = jnp.log(2.0 ** (5.0 + jnp.arange(N_RET_HEADS, dtype=f32)) - 1.0)
    return {
        'x_prompt': nrm(ks[0], (BATCH, SEQ, D), 1.0),
        'x_sample': nrm(ks[1], (DEC_BATCH, DEC_SEQ, D), 1.0),
        'state_ret_fwd': nrm(ks[2], (DEC_BATCH, DEPTH, N_RET_HEADS, RET_DK, RET_DV), 1.0),
        'state_ret_bwd': nrm(ks[3], (DEC_BATCH, DEPTH, N_RET_HEADS, RET_DK, RET_DV), 1.0),
        'c': nrm(ks[4], (DEC_BATCH, D), 1.0),
        'c_ctx': nrm(ks[5], (D,), 1.0),
        'w_ada': nrm(ks[6], (DEPTH, D, N_MOD * D), 0.5 * D ** -0.5),
        'b_ada': nrm(ks[7], (DEPTH, N_MOD * D), 0.01),
        'norm1_g': 1.0 + nrm(ks[8], (DEPTH, D), 0.02),
        'w_in': nrm(ks[9], (DEPTH, D, IN_WIDTH), D ** -0.5),
        'conv_dw_w': nrm(ks[10], (DEPTH, CONV_KERNEL, CONV_WIDTH), CONV_KERNEL ** -0.5),
        'conv_dw_b': nrm(ks[11], (DEPTH, CONV_WIDTH), 0.01),
        'conv_ln_g': 1.0 + nrm(ks[12], (DEPTH, CONV_WIDTH), 0.02),
        'conv_ln_b': nrm(ks[13], (DEPTH, CONV_WIDTH), 0.01),
        'w_conv_out': nrm(ks[14], (DEPTH, CONV_WIDTH, D), CONV_WIDTH ** -0.5),
        'ret_decay_logit': base_logit + nrm(ks[15], (DEPTH, 2, N_RET_HEADS), 0.01),
        'w_ret_out': nrm(ks[16], (DEPTH, RET_V_WIDTH, D), RET_V_WIDTH ** -0.5),
        'w_mix_out': nrm(ks[17], (DEPTH, D, D), D ** -0.5),
        'norm2_g': 1.0 + nrm(ks[18], (DEPTH, D), 0.02),
        'w_ffn_in': nrm(ks[19], (DEPTH, D, 2 * FFN_HIDDEN), D ** -0.5),
        'w_ffn_out': nrm(ks[20], (DEPTH, FFN_HIDDEN, D), FFN_HIDDEN ** -0.5),
        'final_norm_g': 1.0 + nrm(ks[21], (D,), 0.02),
    }


def reference(x_prompt, x_sample, state_ret_fwd, state_ret_bwd, c, c_ctx, w_ada, b_ada, norm1_g, w_in,
              conv_dw_w, conv_dw_b, conv_ln_g, conv_ln_b, w_conv_out, ret_decay_logit, w_ret_out,
              w_mix_out, norm2_g, w_ffn_in, w_ffn_out, final_norm_g):
    n_ctx_req = x_prompt.shape[0]
    zero_state = jnp.zeros((n_ctx_req, N_RET_HEADS, RET_DK, RET_DV), jnp.float32)
    rope = grid_rotary(x_sample.shape[1])
    xp = x_prompt
    xs = x_sample
    new_f = []
    new_b = []
    for l in range(DEPTH):
        layer_w = (w_ada[l], b_ada[l], norm1_g[l], w_in[l], conv_dw_w[l], conv_dw_b[l], conv_ln_g[l],
                   conv_ln_b[l], w_conv_out[l], ret_decay_logit[l], w_ret_out[l], w_mix_out[l],
                   norm2_g[l], w_ffn_in[l], w_ffn_out[l])
        xp, sf, sb = trunk_layer(xp, c_ctx[None, :], None, zero_state, zero_state, *layer_w)
        new_f.append(sf)
        new_b.append(sb)
        xs, _, _ = trunk_layer(xs, c, rope, state_ret_fwd[:, l], state_ret_bwd[:, l], *layer_w)
    y_prompt = rms_norm(xp, final_norm_g)
    y_sample = rms_norm(xs, final_norm_g)
    new_state_ret_fwd = jnp.stack(new_f, axis=1)
    new_state_ret_bwd = jnp.stack(new_b, axis=1)
    return (y_prompt, y_sample, new_state_ret_fwd, new_state_ret_bwd)
```

```python
import functools

import jax
import jax.numpy as jnp
from jax import lax
from jax.experimental import pallas as pl
from jax.experimental.pallas import tpu as pltpu

F32 = jnp.float32
BF16 = jnp.bfloat16

D_MODEL = 2048
CONV_W = 1024
CONV_K = 31
CONV_PAD = CONV_K // 2
N_HEADS = 8
DK = 64
DV = 128
FFN_H = 5632
N_MOD = 6
IN_W = 10240
GRID_W = 64
ROPE_BASE = 10000.0
EPS = 1e-6

LANES = 128
HALO = 16
MIB = 1024 * 1024

CB_Q = 2048 // LANES
CB_K = 2560 // LANES
CB_V = 3072 // LANES
CB_GF = 4096 // LANES
CB_GB = 5120 // LANES

MOD_SHIFT1, MOD_SCALE1, MOD_GATE1, MOD_SHIFT2, MOD_SCALE2, MOD_GATE2 = range(N_MOD)


def _cparams(n_axes, vmem_mib):
    return pltpu.CompilerParams(dimension_semantics=("arbitrary",) * n_axes,
                                vmem_limit_bytes=vmem_mib * MIB)


def _sigmoid(x):
    return jax.nn.sigmoid(x)


def _rms_rows(x, g):
    ms = jnp.mean(x * x, axis=-1, keepdims=True)
    return x * lax.rsqrt(ms + EPS) * g


def _mod_kernel(c_ref, w_ref, b_ref, o_ref):
    c = c_ref[...]
    s = (c * _sigmoid(c)).astype(BF16)
    o_ref[...] = jnp.dot(s, w_ref[...].astype(BF16), preferred_element_type=F32) + b_ref[...]


def _mod_call(cond8, w_ada, b_ada):
    n = w_ada.shape[1]
    tn = 1024
    return pl.pallas_call(
        _mod_kernel,
        grid=(n // tn,),
        in_specs=[pl.BlockSpec((8, D_MODEL), lambda j: (0, 0)),
                  pl.BlockSpec((D_MODEL, tn), lambda j: (0, j)),
                  pl.BlockSpec((1, tn), lambda j: (0, j))],
        out_specs=pl.BlockSpec((8, tn), lambda j: (0, j)),
        out_shape=jax.ShapeDtypeStruct((8, n), F32),
        compiler_params=_cparams(1, 40),
        name="mod",
    )(cond8, w_ada, b_ada.reshape(1, n))


def _inproj_kernel(x_ref, mod_ref, g_ref, w_ref, o_ref, h_ref, *, tm, rc):
    @pl.when(pl.program_id(1) == 0)
    def _():
        g = g_ref[...]
        scale = 1.0 + mod_ref[MOD_SCALE1:MOD_SCALE1 + 1, :]
        shift = mod_ref[MOD_SHIFT1:MOD_SHIFT1 + 1, :]

        def body(i, c):
            r = pl.multiple_of(i * rc, rc)
            x = x_ref[pl.ds(r, rc), :]
            h_ref[pl.ds(r, rc), :] = (_rms_rows(x, g) * scale + shift).astype(BF16)
            return c

        lax.fori_loop(0, tm // rc, body, 0)

    o_ref[...] = jnp.dot(h_ref[...], w_ref[...], preferred_element_type=F32)


def _inproj_call(x2d, mod3, rows_per_mod, norm_g, w_in_bf):
    m = x2d.shape[0]
    tm, tn = 1024, 1024
    kern = functools.partial(_inproj_kernel, tm=tm, rc=64)
    return pl.pallas_call(
        kern,
        grid=(m // tm, IN_W // tn),
        in_specs=[pl.BlockSpec((tm, D_MODEL), lambda i, j: (i, 0)),
                  pl.BlockSpec((None, N_MOD, D_MODEL), lambda i, j: ((i * tm) // rows_per_mod, 0, 0)),
                  pl.BlockSpec((1, D_MODEL), lambda i, j: (0, 0)),
                  pl.BlockSpec((D_MODEL, tn), lambda i, j: (0, j))],
        out_specs=pl.BlockSpec((tm, tn), lambda i, j: (i, j)),
        out_shape=jax.ShapeDtypeStruct((m, IN_W), F32),
        scratch_shapes=[pltpu.VMEM((tm, D_MODEL), BF16)],
        compiler_params=_cparams(2, 48),
        name="inproj",
    )(x2d, mod3, norm_g, w_in_bf)


def _conv_kernel(u_ref, g_ref, up_ref, gp_ref, un_ref, gn_ref, w_ref, b_ref, lg_ref, lb_ref,
                 o_ref, abuf, *, tt, seq_len, rc):
    i = pl.program_id(0)
    nblk = CONV_W // LANES
    at_left = ((i * tt) % seq_len) == 0
    at_right = (((i + 1) * tt) % seq_len) == 0

    a_prev = jnp.where(at_left, 0.0, up_ref[...] * _sigmoid(gp_ref[...]))
    a_next = jnp.where(at_right, 0.0, un_ref[...] * _sigmoid(gn_ref[...]))
    for k in range(nblk):
        abuf[k, 0:HALO, :] = a_prev[:, k * LANES:(k + 1) * LANES]
        abuf[k, HALO + tt:HALO + tt + HALO, :] = a_next[:, k * LANES:(k + 1) * LANES]

    def fill(ci, c):
        r = pl.multiple_of(ci * 64, 64)
        a = u_ref[pl.ds(r, 64), :] * _sigmoid(g_ref[pl.ds(r, 64), :])
        for k in range(nblk):
            abuf[k, pl.ds(HALO + r, 64), :] = a[:, k * LANES:(k + 1) * LANES]
        return c

    lax.fori_loop(0, tt // 64, fill, 0)

    bias = b_ref[...]
    ln_g = lg_ref[...]
    ln_b = lb_ref[...]
    reps = rc // 8

    def rows(ci, c):
        r = pl.multiple_of(ci * rc, rc)
        pieces = []
        for k in range(nblk):
            acc = jnp.zeros((rc, LANES), F32)
            for j in range(CONV_K):
                wj = w_ref[j, :, k * LANES:(k + 1) * LANES]
                win = abuf[k, pl.ds(r + (HALO - CONV_PAD) + j, rc), :]
                acc = acc + win * jnp.concatenate([wj] * reps, axis=0)
            pieces.append(acc)
        y = jnp.concatenate(pieces, axis=-1) + bias
        mu = jnp.mean(y, axis=-1, keepdims=True)
        yc = y - mu
        var = jnp.mean(yc * yc, axis=-1, keepdims=True)
        z = yc * lax.rsqrt(var + EPS) * ln_g + ln_b
        o_ref[pl.ds(r, rc), :] = (z * _sigmoid(z)).astype(BF16)
        return c

    lax.fori_loop(0, tt // rc, rows, 0)


def _conv_call(proj, seq_len, w_b8, bias, ln_g, ln_b):
    m = proj.shape[0]
    tt = 256
    hb = tt // HALO
    last = m // HALO - 1
    kern = functools.partial(_conv_kernel, tt=tt, seq_len=seq_len, rc=32)
    prev = lambda c: (lambda i: (jnp.maximum(i * hb - 1, 0), c))
    nxt = lambda c: (lambda i: (jnp.minimum((i + 1) * hb, last), c))
    return pl.pallas_call(
        kern,
        grid=(m // tt,),
        in_specs=[pl.BlockSpec((tt, CONV_W), lambda i: (i, 0)),
                  pl.BlockSpec((tt, CONV_W), lambda i: (i, 1)),
                  pl.BlockSpec((HALO, CONV_W), prev(0)),
                  pl.BlockSpec((HALO, CONV_W), prev(1)),
                  pl.BlockSpec((HALO, CONV_W), nxt(0)),
                  pl.BlockSpec((HALO, CONV_W), nxt(1)),
                  pl.BlockSpec((CONV_K, 8, CONV_W), lambda i: (0, 0, 0)),
                  pl.BlockSpec((1, CONV_W), lambda i: (0, 0)),
                  pl.BlockSpec((1, CONV_W), lambda i: (0, 0)),
                  pl.BlockSpec((1, CONV_W), lambda i: (0, 0))],
        out_specs=pl.BlockSpec((tt, CONV_W), lambda i: (i, 0)),
        out_shape=jax.ShapeDtypeStruct((m, CONV_W), BF16),
        scratch_shapes=[pltpu.VMEM((CONV_W // LANES, tt + 2 * HALO, LANES), F32)],
        compiler_params=_cparams(1, 32),
        name="conv",
    )(proj, proj, proj, proj, proj, proj, w_b8, bias, ln_g, ln_b)


def _ret_kernel(*refs, seq_len, nb, chunk, has_s0, has_rope, want_state):
    it = iter(refs)
    logit_ref = next(it)
    q_ref, k_ref, v_ref, gf_ref, gb_ref = (next(it) for _ in range(5))
    cos_ref = sin_ref = s0f_ref = s0b_ref = sf_ref = sb_ref = None
    if has_rope:
        cos_ref, sin_ref = next(it), next(it)
    if has_s0:
        s0f_ref, s0b_ref = next(it), next(it)
    o_ref = next(it)
    if want_state:
        sf_ref, sb_ref = next(it), next(it)
    qs, ks, pf, pb, stf, stb = (next(it) for _ in range(6))

    C = chunk
    n_chunks = seq_len // C
    rows = nb * seq_len
    h = pl.program_id(1)
    par = h % 2
    lane = lax.broadcasted_iota(jnp.int32, (1, LANES), 1)
    own = (lane // DK) == par
    first_half = (lane % DK) < (DK // 2)

    def prep(ci, c):
        r = pl.multiple_of(ci * C, C)
        q = q_ref[pl.ds(r, C), :]
        k = k_ref[pl.ds(r, C), :]
        if has_rope:
            t = pl.multiple_of(r % seq_len, C)
            cs = cos_ref[pl.ds(t, C), :]
            sn = sin_ref[pl.ds(t, C), :]

            def rot(x):
                partner = jnp.where(first_half, pltpu.roll(x, LANES - DK // 2, 1), pltpu.roll(x, DK // 2, 1))
                return x * cs + partner * sn

            q = rot(q)
            k = rot(k)
        qs[pl.ds(r, C), :] = jnp.where(own, q, 0.0)
        ks[pl.ds(r, C), :] = k * (DK ** -0.5)
        return c

    lax.fori_loop(0, rows // C, prep, 0)

    n_i = lax.broadcasted_iota(jnp.int32, (C, C), 0).astype(F32)
    m_i = lax.broadcasted_iota(jnp.int32, (C, C), 1).astype(F32)
    row = lax.broadcasted_iota(jnp.int32, (C, LANES), 0).astype(F32)
    lg_f_cc = jax.nn.log_sigmoid(jnp.full((C, C), logit_ref[0, h], F32))
    lg_b_cc = jax.nn.log_sigmoid(jnp.full((C, C), logit_ref[1, h], F32))
    lg_f = jax.nn.log_sigmoid(jnp.full((C, LANES), logit_ref[0, h], F32))
    lg_b = jax.nn.log_sigmoid(jnp.full((C, LANES), logit_ref[1, h], F32))
    dfw = n_i - m_i
    dbw = m_i - n_i
    dec_f = jnp.where(dfw >= 0, jnp.exp(jnp.where(dfw >= 0, dfw, 0.0) * lg_f_cc), 0.0)
    dec_b = jnp.where(dbw >= 0, jnp.exp(jnp.where(dbw >= 0, dbw, 0.0) * lg_b_cc), 0.0)
    qd_f = jnp.exp((row + 1.0) * lg_f)
    kd_f = jnp.exp((C - 1.0 - row) * lg_f)
    qd_b = jnp.exp((C - row) * lg_b)
    kd_b = jnp.exp(row * lg_b)
    cd_f = jnp.exp(C * lg_f[0:1, :])
    cd_b = jnp.exp(C * lg_b[0:1, :])

    def chunk_step(r, dec, qd, kd, cd, st, g_ref, p_ref):
        q = qs[pl.ds(r, C), :]
        k = ks[pl.ds(r, C), :]
        vb = v_ref[pl.ds(r, C), :].astype(BF16)
        raw = lax.dot_general(q.astype(BF16), k.astype(BF16), (((1,), (1,)), ((), ())),
                              preferred_element_type=F32)
        s = st[...]
        lhs = jnp.concatenate([(raw * dec).astype(BF16), (q * qd).astype(BF16)], axis=1)
        rhs = jnp.concatenate([vb, s.astype(BF16)], axis=0)
        o = jnp.dot(lhs, rhs, preferred_element_type=F32)
        upd = lax.dot_general((k * kd).astype(BF16), vb, (((0,), (0,)), ((), ())),
                              preferred_element_type=F32)
        st[...] = cd * s + upd
        ms = jnp.mean(o * o, axis=-1, keepdims=True)
        g = g_ref[pl.ds(r, C), :]
        p_ref[pl.ds(r, C), :] = (g * _sigmoid(g)) * (o * lax.rsqrt(ms + EPS))

    def seq_body(s, c):
        base = s * seq_len
        stf[...] = jnp.zeros((LANES, DV), F32)
        stb[...] = jnp.zeros((LANES, DV), F32)
        if has_s0:
            off = pl.multiple_of(par * DK, DK)
            stf[pl.ds(off, DK), :] = s0f_ref[s]
            stb[pl.ds(off, DK), :] = s0b_ref[s]

        def pair(ci, cc):
            rf = pl.multiple_of(base + ci * C, C)
            rb = pl.multiple_of(base + (n_chunks - 1 - ci) * C, C)
            chunk_step(rf, dec_f, qd_f, kd_f, cd_f, stf, gf_ref, pf)
            chunk_step(rb, dec_b, qd_b, kd_b, cd_b, stb, gb_ref, pb)
            return cc

        lax.fori_loop(0, n_chunks, pair, 0, unroll=2)
        if want_state:
            off = pl.multiple_of(par * DK, DK)
            sf_ref[s] = stf[pl.ds(off, DK), :]
            sb_ref[s] = stb[pl.ds(off, DK), :]
        return c

    lax.fori_loop(0, nb, seq_body, 0)

    def fin(ci, c):
        r = pl.multiple_of(ci * C, C)
        o_ref[pl.ds(r, C), :] = (pf[pl.ds(r, C), :] + pb[pl.ds(r, C), :]).astype(BF16)
        return c

    lax.fori_loop(0, rows // C, fin, 0)


def _ret_call(proj, n_seq, seq_len, decay_logit, rope, s0, want_state):
    m = proj.shape[0]
    rows = 4096
    nb = rows // seq_len
    has_rope = rope is not None
    has_s0 = s0 is not None
    kern = functools.partial(_ret_kernel, seq_len=seq_len, nb=nb, chunk=128, has_s0=has_s0,
                             has_rope=has_rope, want_state=want_state)
    col = lambda cb, per_head: (lambda b, h: (b, cb + (h if per_head else h // 2)))
    in_specs = [pl.BlockSpec(memory_space=pltpu.SMEM),
                pl.BlockSpec((rows, LANES), col(CB_Q, False)),
                pl.BlockSpec((rows, LANES), col(CB_K, False)),
                pl.BlockSpec((rows, LANES), col(CB_V, True)),
                pl.BlockSpec((rows, LANES), col(CB_GF, True)),
                pl.BlockSpec((rows, LANES), col(CB_GB, True))]
    args = [decay_logit, proj, proj, proj, proj, proj]
    if has_rope:
        in_specs += [pl.BlockSpec((seq_len, LANES), lambda b, h: (0, 0))] * 2
        args += list(rope)
    if has_s0:
        in_specs += [pl.BlockSpec((nb, None, DK, DV), lambda b, h: (b, h, 0, 0))] * 2
        args += list(s0)
    out_specs = [pl.BlockSpec((rows, LANES), lambda b, h: (b, h))]
    out_shape = [jax.ShapeDtypeStruct((m, N_HEADS * DV), BF16)]
    if want_state:
        out_specs += [pl.BlockSpec((nb, None, DK, DV), lambda b, h: (b, h, 0, 0))] * 2
        out_shape += [jax.ShapeDtypeStruct((n_seq, N_HEADS, DK, DV), F32)] * 2
    return pl.pallas_call(
        kern,
        grid=(m // rows, N_HEADS),
        in_specs=in_specs,
        out_specs=out_specs,
        out_shape=out_shape,
        scratch_shapes=[pltpu.VMEM((rows, LANES), F32), pltpu.VMEM((rows, LANES), F32),
                        pltpu.VMEM((rows, DV), F32), pltpu.VMEM((rows, DV), F32),
                        pltpu.VMEM((LANES, DV), F32), pltpu.VMEM((LANES, DV), F32)],
        compiler_params=_cparams(2, 48),
        name="ret",
    )(*args)


def _merge_kernel(a_ref, r_ref, ga_ref, gb_ref, wc_ref, wr_ref, o_ref):
    ba = jnp.dot(a_ref[...], wc_ref[...], preferred_element_type=F32)
    bb = jnp.dot(r_ref[...], wr_ref[...], preferred_element_type=F32)
    o_ref[...] = (_sigmoid(ga_ref[...]) * ba + _sigmoid(gb_ref[...]) * bb).astype(BF16)


def _merge_call(a_act, ret, proj, wc_bf, wr_bf):
    m = a_act.shape[0]
    tm, tn = 1024, 1024
    ga0 = 6144 // tn
    gb0 = 8192 // tn
    return pl.pallas_call(
        _merge_kernel,
        grid=(m // tm, D_MODEL // tn),
        in_specs=[pl.BlockSpec((tm, CONV_W), lambda i, j: (i, 0)),
                  pl.BlockSpec((tm, N_HEADS * DV), lambda i, j: (i, 0)),
                  pl.BlockSpec((tm, tn), lambda i, j: (i, ga0 + j)),
                  pl.BlockSpec((tm, tn), lambda i, j: (i, gb0 + j)),
                  pl.BlockSpec((CONV_W, tn), lambda i, j: (0, j)),
                  pl.BlockSpec((N_HEADS * DV, tn), lambda i, j: (0, j))],
        out_specs=pl.BlockSpec((tm, tn), lambda i, j: (i, j)),
        out_shape=jax.ShapeDtypeStruct((m, D_MODEL), BF16),
        compiler_params=_cparams(2, 56),
        name="merge",
    )(a_act, ret, proj, proj, wc_bf, wr_bf)


def _mixout_kernel(m_ref, x_ref, mod_ref, g_ref, w_ref, x1_ref, h2_ref):
    y = jnp.dot(m_ref[...], w_ref[...], preferred_element_type=F32)
    x1 = x_ref[...] + mod_ref[MOD_GATE1:MOD_GATE1 + 1, :] * y
    x1_ref[...] = x1
    scale = 1.0 + mod_ref[MOD_SCALE2:MOD_SCALE2 + 1, :]
    shift = mod_ref[MOD_SHIFT2:MOD_SHIFT2 + 1, :]
    h2_ref[...] = (_rms_rows(x1, g_ref[...]) * scale + shift).astype(BF16)


def _mixout_call(merged, x2d, mod3, rows_per_mod, norm_g, wm_bf):
    m = x2d.shape[0]
    tm = 512
    return pl.pallas_call(
        _mixout_kernel,
        grid=(m // tm,),
        in_specs=[pl.BlockSpec((tm, D_MODEL), lambda i: (i, 0)),
                  pl.BlockSpec((tm, D_MODEL), lambda i: (i, 0)),
                  pl.BlockSpec((None, N_MOD, D_MODEL), lambda i: ((i * tm) // rows_per_mod, 0, 0)),
                  pl.BlockSpec((1, D_MODEL), lambda i: (0, 0)),
                  pl.BlockSpec((D_MODEL, D_MODEL), lambda i: (0, 0))],
        out_specs=[pl.BlockSpec((tm, D_MODEL), lambda i: (i, 0)),
                   pl.BlockSpec((tm, D_MODEL), lambda i: (i, 0))],
        out_shape=[jax.ShapeDtypeStruct((m, D_MODEL), F32),
                   jax.ShapeDtypeStruct((m, D_MODEL), BF16)],
        compiler_params=_cparams(1, 56),
        name="mixout",
    )(merged, x2d, mod3, norm_g, wm_bf)


def _ffn_kernel(h_ref, x1_ref, mod_ref, fg_ref, wg_ref, wu_ref, wo_ref, y_ref, acc_ref, *, n_steps,
                final_norm):
    j = pl.program_id(1)
    h = h_ref[...]
    gate = jnp.dot(h, wg_ref[...], preferred_element_type=F32)
    up = jnp.dot(h, wu_ref[...], preferred_element_type=F32)
    t = ((gate * _sigmoid(gate)) * up).astype(BF16)
    contrib = jnp.dot(t, wo_ref[...], preferred_element_type=F32)

    @pl.when(j == 0)
    def _():
        acc_ref[...] = contrib

    @pl.when(j > 0)
    def _():
        acc_ref[...] += contrib

    @pl.when(j == n_steps - 1)
    def _():
        x2 = x1_ref[...] + mod_ref[MOD_GATE2:MOD_GATE2 + 1, :] * acc_ref[...]
        y_ref[...] = _rms_rows(x2, fg_ref[...]) if final_norm else x2


def _ffn_call(h2, x1, mod3, rows_per_mod, final_g, final_norm, wi_bf, wo_bf):
    m = h2.shape[0]
    tm, th = 512, 512
    n_steps = FFN_H // th
    kern = functools.partial(_ffn_kernel, n_steps=n_steps, final_norm=final_norm)
    return pl.pallas_call(
        kern,
        grid=(m // tm, n_steps),
        in_specs=[pl.BlockSpec((tm, D_MODEL), lambda i, j: (i, 0)),
                  pl.BlockSpec((tm, D_MODEL), lambda i, j: (i, 0)),
                  pl.BlockSpec((None, N_MOD, D_MODEL), lambda i, j: ((i * tm) // rows_per_mod, 0, 0)),
                  pl.BlockSpec((1, D_MODEL), lambda i, j: (0, 0)),
                  pl.BlockSpec((D_MODEL, th), lambda i, j: (0, j)),
                  pl.BlockSpec((D_MODEL, th), lambda i, j: (0, n_steps + j)),
                  pl.BlockSpec((th, D_MODEL), lambda i, j: (j, 0))],
        out_specs=pl.BlockSpec((tm, D_MODEL), lambda i, j: (i, 0)),
        out_shape=jax.ShapeDtypeStruct((m, D_MODEL), F32),
        scratch_shapes=[pltpu.VMEM((tm, D_MODEL), F32)],
        compiler_params=_cparams(2, 56),
        name="ffn",
    )(h2, x1, mod3, final_g, wi_bf, wi_bf, wo_bf)


def _rope_tables(n_tokens):
    rows = n_tokens // GRID_W
    row = jnp.repeat(jnp.arange(rows, dtype=F32), GRID_W)
    col = jnp.tile(jnp.arange(GRID_W, dtype=F32), rows)
    n_freq = DK // 4
    inv = ROPE_BASE ** (-jnp.arange(n_freq, dtype=F32) / n_freq)
    ang = jnp.concatenate([row[:, None] * inv, col[:, None] * inv], axis=-1)
    cos, sin = jnp.cos(ang), jnp.sin(ang)
    reps = LANES // DK
    return (jnp.concatenate([cos, cos] * reps, axis=-1), jnp.concatenate([-sin, sin] * reps, axis=-1))


def _trunk_layer(x2d, n_seq, seq_len, mod3, rows_per_mod, rope, s0, want_state, w):
    proj = _inproj_call(x2d, mod3, rows_per_mod, w["norm1_g"], w["w_in"])
    a_act = _conv_call(proj, seq_len, w["conv_w8"], w["conv_b"], w["conv_ln_g"], w["conv_ln_b"])
    ret = _ret_call(proj, n_seq, seq_len, w["decay_logit"], rope, s0, want_state)
    ret, states = (ret[0], ret[1:])
    merged = _merge_call(a_act, ret, proj, w["w_conv_out"], w["w_ret_out"])
    x1, h2 = _mixout_call(merged, x2d, mod3, rows_per_mod, w["norm2_g"], w["w_mix_out"])
    return x1, h2, states


def kernel(x_prompt, x_sample, state_ret_fwd, state_ret_bwd, c, c_ctx, w_ada, b_ada, norm1_g, w_in,
           conv_dw_w, conv_dw_b, conv_ln_g, conv_ln_b, w_conv_out, ret_decay_logit, w_ret_out, w_mix_out,
           norm2_g, w_ffn_in, w_ffn_out, final_norm_g):
    depth = w_ada.shape[0]
    bp, tp, _ = x_prompt.shape
    bs, ts, _ = x_sample.shape
    n_cond = 1 + bs
    cond8 = jnp.concatenate([c_ctx[None, :], c, jnp.zeros((8 - n_cond, D_MODEL), F32)], axis=0)
    rope = _rope_tables(ts)
    final_g = final_norm_g.reshape(1, D_MODEL)

    xp = x_prompt.reshape(bp * tp, D_MODEL)
    xs = x_sample.reshape(bs * ts, D_MODEL)
    new_f, new_b = [], []
    for l in range(depth):
        mod = _mod_call(cond8, w_ada[l], b_ada[l])
        mod3 = mod.reshape(8, N_MOD, D_MODEL)
        w = {
            "norm1_g": norm1_g[l].reshape(1, D_MODEL),
            "w_in": w_in[l].astype(BF16),
            "conv_w8": jnp.broadcast_to(conv_dw_w[l][:, None, :], (CONV_K, 8, CONV_W)),
            "conv_b": conv_dw_b[l].reshape(1, CONV_W),
            "conv_ln_g": conv_ln_g[l].reshape(1, CONV_W),
            "conv_ln_b": conv_ln_b[l].reshape(1, CONV_W),
            "w_conv_out": w_conv_out[l].astype(BF16),
            "decay_logit": ret_decay_logit[l],
            "w_ret_out": w_ret_out[l].astype(BF16),
            "w_mix_out": w_mix_out[l].astype(BF16),
            "norm2_g": norm2_g[l].reshape(1, D_MODEL),
        }
        wi_bf = w_ffn_in[l].astype(BF16)
        wo_bf = w_ffn_out[l].astype(BF16)
        last = l == depth - 1

        x1p, h2p, (sf, sb) = _trunk_layer(xp, bp, tp, mod3[0:1], bp * tp, None, None, True, w)
        x1s, h2s, _ = _trunk_layer(xs, bs, ts, mod3[1:n_cond], ts, rope,
                                   (state_ret_fwd[:, l], state_ret_bwd[:, l]), False, w)
        new_f.append(sf)
        new_b.append(sb)
        xp = _ffn_call(h2p, x1p, mod3[0:1], bp * tp, final_g, last, wi_bf, wo_bf)
        xs = _ffn_call(h2s, x1s, mod3[1:n_cond], ts, final_g, last, wi_bf, wo_bf)

    y_prompt = xp.reshape(bp, tp, D_MODEL)
    y_sample = xs.reshape(bs, ts, D_MODEL)
    return (y_prompt, y_sample, jnp.stack(new_f, axis=1), jnp.stack(new_b, axis=1))
```

```python
import functools

import jax
import jax.numpy as jnp
from jax import lax
from jax.experimental import pallas as pl
from jax.experimental.pallas import tpu as pltpu

F32 = jnp.float32
BF16 = jnp.bfloat16

D_MODEL = 2048
CONV_W = 1024
CONV_K = 31
CONV_PAD = CONV_K // 2
N_HEADS = 8
DK = 64
DV = 128
FFN_H = 5632
N_MOD = 6
IN_W = 10240
GRID_W = 64
ROPE_BASE = 10000.0
EPS = 1e-6

LANES = 128
HALO = 16
MIB = 1024 * 1024

COL_Q = 2048
COL_K = 2560
COL_V = 3072
COL_GF = 4096
COL_GB = 5120
COL_GA = 6144
COL_GBR = 8192

MOD_SHIFT1, MOD_SCALE1, MOD_GATE1, MOD_SHIFT2, MOD_SCALE2, MOD_GATE2 = range(N_MOD)


def _cparams(n_axes, vmem_mib):
    return pltpu.CompilerParams(dimension_semantics=("arbitrary",) * n_axes,
                                vmem_limit_bytes=vmem_mib * MIB)


def _sigmoid(x):
    return 0.5 * jnp.tanh(0.5 * x) + 0.5


def _rms_rows(x, g):
    ms = jnp.mean(x * x, axis=-1, keepdims=True)
    return x * lax.rsqrt(ms + EPS) * g


def _row_chunks(n_rows, rc, fn, unroll=1):
    def body(i, c):
        fn(pl.multiple_of(i * rc, rc))
        return c

    lax.fori_loop(0, n_rows // rc, body, 0, unroll=unroll)


def _mod_kernel(c_ref, w_ref, b_ref, o_ref):
    c = c_ref[...]
    s = (c * _sigmoid(c)).astype(BF16)
    o_ref[...] = jnp.dot(s, w_ref[...].astype(BF16), preferred_element_type=F32) + b_ref[...]


def _mod_call(cond8, w_ada, b_ada):
    n = w_ada.shape[1]
    tn = 1024
    return pl.pallas_call(
        _mod_kernel,
        grid=(n // tn,),
        in_specs=[pl.BlockSpec((8, D_MODEL), lambda j: (0, 0)),
                  pl.BlockSpec((D_MODEL, tn), lambda j: (0, j)),
                  pl.BlockSpec((1, tn), lambda j: (0, j))],
        out_specs=pl.BlockSpec((8, tn), lambda j: (0, j)),
        out_shape=jax.ShapeDtypeStruct((8, n), F32),
        compiler_params=_cparams(1, 40),
        name="mod",
    )(cond8, w_ada, b_ada.reshape(1, n))


def _rotate_pairs(x, cs, sn):
    lane = lax.broadcasted_iota(jnp.int32, (1, LANES), 1)
    first_half = (lane % DK) < (DK // 2)
    partner = jnp.where(first_half, pltpu.roll(x, LANES - DK // 2, 1), pltpu.roll(x, DK // 2, 1))
    return x * cs + partner * sn


def _inproj_kernel(*refs, tm, tn, has_rope):
    if has_rope:
        x_ref, mod_ref, g_ref, w_ref, cos_ref, sin_ref, o_ref, h_ref = refs
    else:
        x_ref, mod_ref, g_ref, w_ref, o_ref, h_ref = refs
    j = pl.program_id(1)

    @pl.when(j == 0)
    def _():
        g = g_ref[...]
        scale = 1.0 + mod_ref[MOD_SCALE1:MOD_SCALE1 + 1, :]
        shift = mod_ref[MOD_SHIFT1:MOD_SHIFT1 + 1, :]

        def chunk(r):
            x = x_ref[pl.ds(r, 16), :]
            h_ref[pl.ds(r, 16), :] = (_rms_rows(x, g) * scale + shift).astype(BF16)

        _row_chunks(tm, 16, chunk, unroll=4)

    o_ref[...] = jnp.dot(h_ref[...], w_ref[...], preferred_element_type=F32)

    if has_rope:
        @pl.when(j == COL_Q // tn)
        def _():
            def chunk(r):
                cs = cos_ref[pl.ds(r, 32), :]
                sn = sin_ref[pl.ds(r, 32), :]
                for b in range(tn // LANES):
                    sl = slice(b * LANES, (b + 1) * LANES)
                    o_ref[pl.ds(r, 32), sl] = _rotate_pairs(o_ref[pl.ds(r, 32), sl], cs, sn)

            _row_chunks(tm, 32, chunk, unroll=2)


def _inproj_call(x2d, mod3, rows_per_mod, norm_g, w_in_bf, rope, seq_len):
    m = x2d.shape[0]
    tm, tn = 1024, 1024
    assert COL_K + N_HEADS * DK == COL_Q + tn and COL_Q % tn == 0
    has_rope = rope is not None
    kern = functools.partial(_inproj_kernel, tm=tm, tn=tn, has_rope=has_rope)
    in_specs = [pl.BlockSpec((tm, D_MODEL), lambda i, j: (i, 0)),
                pl.BlockSpec((None, N_MOD, D_MODEL), lambda i, j: ((i * tm) // rows_per_mod, 0, 0)),
                pl.BlockSpec((1, D_MODEL), lambda i, j: (0, 0)),
                pl.BlockSpec((D_MODEL, tn), lambda i, j: (0, j))]
    args = [x2d, mod3, norm_g, w_in_bf]
    if has_rope:
        in_specs += [pl.BlockSpec((tm, LANES), lambda i, j: (i % (seq_len // tm), 0))] * 2
        args += list(rope)
    return pl.pallas_call(
        kern,
        grid=(m // tm, IN_W // tn),
        in_specs=in_specs,
        out_specs=pl.BlockSpec((tm, tn), lambda i, j: (i, j)),
        out_shape=jax.ShapeDtypeStruct((m, IN_W), F32),
        scratch_shapes=[pltpu.VMEM((tm, D_MODEL), BF16)],
        compiler_params=_cparams(2, 48),
        name="inproj",
    )(*args)


def _conv_kernel(u_ref, g_ref, up_ref, gp_ref, un_ref, gn_ref, w_ref, b_ref, lg_ref, lb_ref,
                 o_ref, abuf, *, tt, seq_len, rc):
    i = pl.program_id(0)
    nblk = CONV_W // LANES
    at_left = ((i * tt) % seq_len) == 0
    at_right = (((i + 1) * tt) % seq_len) == 0

    a_prev = jnp.where(at_left, 0.0, up_ref[...] * _sigmoid(gp_ref[...]))
    a_next = jnp.where(at_right, 0.0, un_ref[...] * _sigmoid(gn_ref[...]))
    for k in range(nblk):
        abuf[k, 0:HALO, :] = a_prev[:, k * LANES:(k + 1) * LANES]
        abuf[k, HALO + tt:HALO + tt + HALO, :] = a_next[:, k * LANES:(k + 1) * LANES]

    def fill(r):
        a = u_ref[pl.ds(r, 32), :] * _sigmoid(g_ref[pl.ds(r, 32), :])
        for k in range(nblk):
            abuf[k, pl.ds(HALO + r, 32), :] = a[:, k * LANES:(k + 1) * LANES]

    _row_chunks(tt, 32, fill, unroll=2)

    bias = b_ref[...]
    ln_g = lg_ref[...]
    ln_b = lb_ref[...]
    reps = rc // 8

    def rows(r):
        pieces = []
        for k in range(nblk):
            acc = jnp.zeros((rc, LANES), F32)
            for j in range(CONV_K):
                wj = w_ref[j, :, k * LANES:(k + 1) * LANES]
                win = abuf[k, pl.ds(r + (HALO - CONV_PAD) + j, rc), :]
                acc = acc + win * jnp.concatenate([wj] * reps, axis=0)
            pieces.append(acc)
        y = jnp.concatenate(pieces, axis=-1) + bias
        mu = jnp.mean(y, axis=-1, keepdims=True)
        yc = y - mu
        var = jnp.mean(yc * yc, axis=-1, keepdims=True)
        z = yc * lax.rsqrt(var + EPS) * ln_g + ln_b
        o_ref[pl.ds(r, rc), :] = (z * _sigmoid(z)).astype(BF16)

    _row_chunks(tt, rc, rows, unroll=2)


def _conv_call(proj, seq_len, w_b8, bias, ln_g, ln_b):
    m = proj.shape[0]
    tt = 256
    hb = tt // HALO
    last = m // HALO - 1
    kern = functools.partial(_conv_kernel, tt=tt, seq_len=seq_len, rc=32)
    prev = lambda c: (lambda i: (jnp.maximum(i * hb - 1, 0), c))
    nxt = lambda c: (lambda i: (jnp.minimum((i + 1) * hb, last), c))
    return pl.pallas_call(
        kern,
        grid=(m // tt,),
        in_specs=[pl.BlockSpec((tt, CONV_W), lambda i: (i, 0)),
                  pl.BlockSpec((tt, CONV_W), lambda i: (i, 1)),
                  pl.BlockSpec((HALO, CONV_W), prev(0)),
                  pl.BlockSpec((HALO, CONV_W), prev(1)),
                  pl.BlockSpec((HALO, CONV_W), nxt(0)),
                  pl.BlockSpec((HALO, CONV_W), nxt(1)),
                  pl.BlockSpec((CONV_K, 8, CONV_W), lambda i: (0, 0, 0)),
                  pl.BlockSpec((1, CONV_W), lambda i: (0, 0)),
                  pl.BlockSpec((1, CONV_W), lambda i: (0, 0)),
                  pl.BlockSpec((1, CONV_W), lambda i: (0, 0))],
        out_specs=pl.BlockSpec((tt, CONV_W), lambda i: (i, 0)),
        out_shape=jax.ShapeDtypeStruct((m, CONV_W), BF16),
        scratch_shapes=[pltpu.VMEM((CONV_W // LANES, tt + 2 * HALO, LANES), F32)],
        compiler_params=_cparams(1, 32),
        name="conv",
    )(proj, proj, proj, proj, proj, proj, w_b8, bias, ln_g, ln_b)


def _ret_kernel(*refs, seq_len, nb, chunk, seqs_per_iter, chunks_per_iter, has_s0, want_state):
    it = iter(refs)
    logit_ref = next(it)
    q_ref, k_ref, v_ref, gf_ref, gb_ref = (next(it) for _ in range(5))
    s0f_ref = s0b_ref = sf_ref = sb_ref = None
    if has_s0:
        s0f_ref, s0b_ref = next(it), next(it)
    o_ref = next(it)
    if want_state:
        sf_ref, sb_ref = next(it), next(it)
    pf, pb = next(it), next(it)

    C = chunk
    n_chunks = seq_len // C
    rows = nb * seq_len
    h = pl.program_id(1)
    par = h % 2
    lane = lax.broadcasted_iota(jnp.int32, (1, LANES), 1)
    own = (lane // DK) == par
    own_rows = (lax.broadcasted_iota(jnp.int32, (LANES, 1), 0) // DK) == par

    kscale = DK ** -0.5
    n_i = lax.broadcasted_iota(jnp.int32, (C, C), 0).astype(F32)
    m_i = lax.broadcasted_iota(jnp.int32, (C, C), 1).astype(F32)
    row = lax.broadcasted_iota(jnp.int32, (C, LANES), 0).astype(F32)
    lg_f_cc = jax.nn.log_sigmoid(jnp.full((C, C), logit_ref[0, h], F32))
    lg_b_cc = jax.nn.log_sigmoid(jnp.full((C, C), logit_ref[1, h], F32))
    lg_f = jax.nn.log_sigmoid(jnp.full((C, LANES), logit_ref[0, h], F32))
    lg_b = jax.nn.log_sigmoid(jnp.full((C, LANES), logit_ref[1, h], F32))
    dfw = n_i - m_i
    dbw = m_i - n_i
    dec_f = jnp.where(dfw >= 0, jnp.exp(jnp.where(dfw >= 0, dfw, 0.0) * lg_f_cc), 0.0) * kscale
    dec_b = jnp.where(dbw >= 0, jnp.exp(jnp.where(dbw >= 0, dbw, 0.0) * lg_b_cc), 0.0) * kscale
    tabs = {
        "f": (dec_f, jnp.exp((row + 1.0) * lg_f), jnp.exp((C - 1.0 - row) * lg_f) * kscale,
              jnp.exp(C * lg_f[0:1, :])),
        "b": (dec_b, jnp.exp((C - row) * lg_b), jnp.exp(row * lg_b) * kscale,
              jnp.exp(C * lg_b[0:1, :])),
    }

    def scores(q, k, v, direction):
        kd = tabs[direction][2]
        qm = jnp.where(own, q, 0.0)
        vb = v.astype(BF16)
        raw = lax.dot_general(qm.astype(BF16), k.astype(BF16), (((1,), (1,)), ((), ())),
                              preferred_element_type=F32)
        upd = lax.dot_general((k * kd).astype(BF16), vb, (((0,), (0,)), ((), ())),
                              preferred_element_type=F32)
        return qm, vb, raw, upd

    def output(qm, vb, raw, state, direction):
        dec, qd = tabs[direction][:2]
        lhs = jnp.concatenate([(raw * dec).astype(BF16), (qm * qd).astype(BF16)], axis=1)
        rhs = jnp.concatenate([vb, state.astype(BF16)], axis=0)
        return jnp.dot(lhs, rhs, preferred_element_type=F32)

    def gated_norm(o, g):
        ms = jnp.mean(o * o, axis=-1, keepdims=True)
        return (g * _sigmoid(g)) * (o * lax.rsqrt(ms + EPS))

    def init_state(s0_ref, s):
        if not has_s0:
            return jnp.zeros((LANES, DV), F32)
        s0 = s0_ref[s]
        return jnp.where(own_rows, jnp.concatenate([s0] * (LANES // DK), axis=0), 0.0)

    def seq_group(sg, c):
        seqs = [sg * seqs_per_iter + u for u in range(seqs_per_iter)]
        init = tuple(init_state(ref, s) for s in seqs for ref in (s0f_ref, s0b_ref))

        def body(itr, states):
            states = list(states)
            tasks = []
            for si, s in enumerate(seqs):
                for u in range(chunks_per_iter):
                    ci = itr * chunks_per_iter + u
                    rf = pl.multiple_of(s * seq_len + ci * C, C)
                    rb = pl.multiple_of(s * seq_len + (n_chunks - 1 - ci) * C, C)
                    tasks.append((2 * si, "f", rf, gf_ref, pf))
                    tasks.append((2 * si + 1, "b", rb, gb_ref, pb))
            loaded = [(q_ref[pl.ds(r, C), :], k_ref[pl.ds(r, C), :], v_ref[pl.ds(r, C), :],
                       g_ref[pl.ds(r, C), :]) for (_, _, r, g_ref, _) in tasks]
            pre = [scores(q, k, v, task[1]) for task, (q, k, v, _) in zip(tasks, loaded)]
            outs = []
            for (slot, direction, _, _, _), (qm, vb, raw, upd) in zip(tasks, pre):
                outs.append(output(qm, vb, raw, states[slot], direction))
                states[slot] = tabs[direction][3] * states[slot] + upd
            for (_, _, r, _, p_ref), o, (_, _, _, g) in zip(tasks, outs, loaded):
                p_ref[pl.ds(r, C), :] = gated_norm(o, g)
            return tuple(states)

        final = lax.fori_loop(0, n_chunks // chunks_per_iter, body, init)
        if want_state:
            for si, s in enumerate(seqs):
                sf_ref[s] = jnp.where(par == 0, final[2 * si][:DK], final[2 * si][DK:])
                sb_ref[s] = jnp.where(par == 0, final[2 * si + 1][:DK], final[2 * si + 1][DK:])
        return c

    lax.fori_loop(0, nb // seqs_per_iter, seq_group, 0)

    def fin(r):
        o_ref[pl.ds(r, 64), :] = (pf[pl.ds(r, 64), :] + pb[pl.ds(r, 64), :]).astype(BF16)

    _row_chunks(rows, 64, fin)


def _ret_call(proj, n_seq, seq_len, decay_logit, s0, want_state):
    m = proj.shape[0]
    rows = 4096
    chunk = 128
    nb = rows // seq_len
    n_chunks = seq_len // chunk
    chunks_per_iter = min(n_chunks, 4)
    seqs_per_iter = max(1, min(nb, 4 // chunks_per_iter))
    has_s0 = s0 is not None
    kern = functools.partial(_ret_kernel, seq_len=seq_len, nb=nb, chunk=chunk, seqs_per_iter=seqs_per_iter,
                             chunks_per_iter=chunks_per_iter, has_s0=has_s0, want_state=want_state)
    col = lambda c0, per_head: (lambda b, h: (b, c0 // LANES + (h if per_head else h // 2)))
    in_specs = [pl.BlockSpec(memory_space=pltpu.SMEM),
                pl.BlockSpec((rows, LANES), col(COL_Q, False)),
                pl.BlockSpec((rows, LANES), col(COL_K, False)),
                pl.BlockSpec((rows, LANES), col(COL_V, True)),
                pl.BlockSpec((rows, LANES), col(COL_GF, True)),
                pl.BlockSpec((rows, LANES), col(COL_GB, True))]
    args = [decay_logit, proj, proj, proj, proj, proj]
    if has_s0:
        in_specs += [pl.BlockSpec((nb, None, DK, DV), lambda b, h: (b, h, 0, 0))] * 2
        args += list(s0)
    out_specs = [pl.BlockSpec((rows, LANES), lambda b, h: (b, h))]
    out_shape = [jax.ShapeDtypeStruct((m, N_HEADS * DV), BF16)]
    if want_state:
        out_specs += [pl.BlockSpec((nb, None, DK, DV), lambda b, h: (b, h, 0, 0))] * 2
        out_shape += [jax.ShapeDtypeStruct((n_seq, N_HEADS, DK, DV), F32)] * 2
    return pl.pallas_call(
        kern,
        grid=(m // rows, N_HEADS),
        in_specs=in_specs,
        out_specs=out_specs,
        out_shape=out_shape,
        scratch_shapes=[pltpu.VMEM((rows, DV), F32), pltpu.VMEM((rows, DV), F32)],
        compiler_params=_cparams(2, 40),
        name="ret",
    )(*args)


def _merge_kernel(a_ref, r_ref, ga_ref, gb_ref, wc_ref, wr_ref, o_ref, ba_ref, bb_ref, *, tm, part, rc):
    a = a_ref[...]
    r = r_ref[...]
    for p in range(D_MODEL // part):
        cs = slice(p * part, (p + 1) * part)
        ba_ref[:, cs] = jnp.dot(a, wc_ref[:, cs], preferred_element_type=F32)
        bb_ref[:, cs] = jnp.dot(r, wr_ref[:, cs], preferred_element_type=F32)
        for c in range(tm // rc):
            rs = slice(c * rc, (c + 1) * rc)
            o_ref[rs, cs] = (_sigmoid(ga_ref[rs, cs]) * ba_ref[rs, cs]
                             + _sigmoid(gb_ref[rs, cs]) * bb_ref[rs, cs]).astype(BF16)


def _merge_call(a_act, ret, proj, wc_bf, wr_bf):
    m = a_act.shape[0]
    tm = 512
    ga0 = COL_GA // D_MODEL
    gb0 = COL_GBR // D_MODEL
    return pl.pallas_call(
        functools.partial(_merge_kernel, tm=tm, part=512, rc=32),
        grid=(m // tm,),
        in_specs=[pl.BlockSpec((tm, CONV_W), lambda i: (i, 0)),
                  pl.BlockSpec((tm, N_HEADS * DV), lambda i: (i, 0)),
                  pl.BlockSpec((tm, D_MODEL), lambda i: (i, ga0)),
                  pl.BlockSpec((tm, D_MODEL), lambda i: (i, gb0)),
                  pl.BlockSpec((CONV_W, D_MODEL), lambda i: (0, 0)),
                  pl.BlockSpec((N_HEADS * DV, D_MODEL), lambda i: (0, 0))],
        out_specs=pl.BlockSpec((tm, D_MODEL), lambda i: (i, 0)),
        out_shape=jax.ShapeDtypeStruct((m, D_MODEL), BF16),
        scratch_shapes=[pltpu.VMEM((tm, D_MODEL), F32), pltpu.VMEM((tm, D_MODEL), F32)],
        compiler_params=_cparams(1, 56),
        name="merge",
    )(a_act, ret, proj, proj, wc_bf, wr_bf)


def _mixout_kernel(m_ref, x_ref, mod_ref, g_ref, w_ref, x1_ref, h2_ref, y_ref, *, tm):
    y_ref[...] = jnp.dot(m_ref[...], w_ref[...], preferred_element_type=F32)
    gate = mod_ref[MOD_GATE1:MOD_GATE1 + 1, :]
    scale = 1.0 + mod_ref[MOD_SCALE2:MOD_SCALE2 + 1, :]
    shift = mod_ref[MOD_SHIFT2:MOD_SHIFT2 + 1, :]
    g = g_ref[...]

    def chunk(r):
        rs = pl.ds(r, 16)
        x1 = x_ref[rs, :] + gate * y_ref[rs, :]
        x1_ref[rs, :] = x1
        h2_ref[rs, :] = (_rms_rows(x1, g) * scale + shift).astype(BF16)

    _row_chunks(tm, 16, chunk, unroll=4)


def _mixout_call(merged, x2d, mod3, rows_per_mod, norm_g, wm_bf):
    m = x2d.shape[0]
    tm = 512
    return pl.pallas_call(
        functools.partial(_mixout_kernel, tm=tm),
        grid=(m // tm,),
        in_specs=[pl.BlockSpec((tm, D_MODEL), lambda i: (i, 0)),
                  pl.BlockSpec((tm, D_MODEL), lambda i: (i, 0)),
                  pl.BlockSpec((None, N_MOD, D_MODEL), lambda i: ((i * tm) // rows_per_mod, 0, 0)),
                  pl.BlockSpec((1, D_MODEL), lambda i: (0, 0)),
                  pl.BlockSpec((D_MODEL, D_MODEL), lambda i: (0, 0))],
        out_specs=[pl.BlockSpec((tm, D_MODEL), lambda i: (i, 0)),
                   pl.BlockSpec((tm, D_MODEL), lambda i: (i, 0))],
        out_shape=[jax.ShapeDtypeStruct((m, D_MODEL), F32),
                   jax.ShapeDtypeStruct((m, D_MODEL), BF16)],
        scratch_shapes=[pltpu.VMEM((tm, D_MODEL), F32)],
        compiler_params=_cparams(1, 56),
        name="mixout",
    )(merged, x2d, mod3, norm_g, wm_bf)


def _ffn_kernel(h_ref, x1_ref, mod_ref, fg_ref, wg_ref, wu_ref, wo_ref, y_ref, acc_ref, gate_ref, up_ref,
                t_ref, *, tm, th, part, rc, n_steps, final_norm):
    j = pl.program_id(1)

    @pl.when(j == 0)
    def _():
        acc_ref[...] = jnp.zeros_like(acc_ref)

    h = h_ref[...]
    for p in range(th // part):
        cs = slice(p * part, (p + 1) * part)
        gate_ref[:, cs] = jnp.dot(h, wg_ref[:, cs], preferred_element_type=F32)
        up_ref[:, cs] = jnp.dot(h, wu_ref[:, cs], preferred_element_type=F32)
        for c in range(tm // rc):
            rs = slice(c * rc, (c + 1) * rc)
            gate = gate_ref[rs, cs]
            t_ref[rs, cs] = ((gate * _sigmoid(gate)) * up_ref[rs, cs]).astype(BF16)
    acc_ref[...] += jnp.dot(t_ref[...], wo_ref[...], preferred_element_type=F32)

    @pl.when(j == n_steps - 1)
    def _():
        gate2 = mod_ref[MOD_GATE2:MOD_GATE2 + 1, :]
        fg = fg_ref[...]

        def out(r):
            rs = pl.ds(r, 16)
            x2 = x1_ref[rs, :] + gate2 * acc_ref[rs, :]
            y_ref[rs, :] = _rms_rows(x2, fg) if final_norm else x2

        _row_chunks(tm, 16, out, unroll=4)


def _ffn_call(h2, x1, mod3, rows_per_mod, final_g, final_norm, wi_bf, wo_bf):
    m = h2.shape[0]
    tm, th = 512, 512
    n_steps = FFN_H // th
    kern = functools.partial(_ffn_kernel, tm=tm, th=th, part=256, rc=64, n_steps=n_steps,
                             final_norm=final_norm)
    return pl.pallas_call(
        kern,
        grid=(m // tm, n_steps),
        in_specs=[pl.BlockSpec((tm, D_MODEL), lambda i, j: (i, 0)),
                  pl.BlockSpec((tm, D_MODEL), lambda i, j: (i, 0)),
                  pl.BlockSpec((None, N_MOD, D_MODEL), lambda i, j: ((i * tm) // rows_per_mod, 0, 0)),
                  pl.BlockSpec((1, D_MODEL), lambda i, j: (0, 0)),
                  pl.BlockSpec((D_MODEL, th), lambda i, j: (0, j)),
                  pl.BlockSpec((D_MODEL, th), lambda i, j: (0, n_steps + j)),
                  pl.BlockSpec((th, D_MODEL), lambda i, j: (j, 0))],
        out_specs=pl.BlockSpec((tm, D_MODEL), lambda i, j: (i, 0)),
        out_shape=jax.ShapeDtypeStruct((m, D_MODEL), F32),
        scratch_shapes=[pltpu.VMEM((tm, D_MODEL), F32), pltpu.VMEM((tm, th), F32),
                        pltpu.VMEM((tm, th), F32), pltpu.VMEM((tm, th), BF16)],
        compiler_params=_cparams(2, 56),
        name="ffn",
    )(h2, x1, mod3, final_g, wi_bf, wi_bf, wo_bf)


def _rope_tables(n_tokens):
    rows = n_tokens // GRID_W
    row = jnp.repeat(jnp.arange(rows, dtype=F32), GRID_W)
    col = jnp.tile(jnp.arange(GRID_W, dtype=F32), rows)
    n_freq = DK // 4
    inv = ROPE_BASE ** (-jnp.arange(n_freq, dtype=F32) / n_freq)
    ang = jnp.concatenate([row[:, None] * inv, col[:, None] * inv], axis=-1)
    cos, sin = jnp.cos(ang), jnp.sin(ang)
    reps = LANES // DK
    return (jnp.concatenate([cos, cos] * reps, axis=-1), jnp.concatenate([-sin, sin] * reps, axis=-1))


def _trunk_layer(x2d, n_seq, seq_len, mod3, rows_per_mod, rope, s0, want_state, w):
    proj = _inproj_call(x2d, mod3, rows_per_mod, w["norm1_g"], w["w_in"], rope, seq_len)
    a_act = _conv_call(proj, seq_len, w["conv_w8"], w["conv_b"], w["conv_ln_g"], w["conv_ln_b"])
    ret = _ret_call(proj, n_seq, seq_len, w["decay_logit"], s0, want_state)
    ret, states = (ret[0], ret[1:])
    merged = _merge_call(a_act, ret, proj, w["w_conv_out"], w["w_ret_out"])
    x1, h2 = _mixout_call(merged, x2d, mod3, rows_per_mod, w["norm2_g"], w["w_mix_out"])
    return x1, h2, states


def kernel(x_prompt, x_sample, state_ret_fwd, state_ret_bwd, c, c_ctx, w_ada, b_ada, norm1_g, w_in,
           conv_dw_w, conv_dw_b, conv_ln_g, conv_ln_b, w_conv_out, ret_decay_logit, w_ret_out, w_mix_out,
           norm2_g, w_ffn_in, w_ffn_out, final_norm_g):
    depth = w_ada.shape[0]
    bp, tp, _ = x_prompt.shape
    bs, ts, _ = x_sample.shape
    n_cond = 1 + bs
    cond8 = jnp.concatenate([c_ctx[None, :], c, jnp.zeros((8 - n_cond, D_MODEL), F32)], axis=0)
    rope = _rope_tables(ts)
    final_g = final_norm_g.reshape(1, D_MODEL)

    xp = x_prompt.reshape(bp * tp, D_MODEL)
    xs = x_sample.reshape(bs * ts, D_MODEL)
    new_f, new_b = [], []
    for l in range(depth):
        mod = _mod_call(cond8, w_ada[l], b_ada[l])
        mod3 = mod.reshape(8, N_MOD, D_MODEL)
        w = {
            "norm1_g": norm1_g[l].reshape(1, D_MODEL),
            "w_in": w_in[l].astype(BF16),
            "conv_w8": jnp.broadcast_to(conv_dw_w[l][:, None, :], (CONV_K, 8, CONV_W)),
            "conv_b": conv_dw_b[l].reshape(1, CONV_W),
            "conv_ln_g": conv_ln_g[l].reshape(1, CONV_W),
            "conv_ln_b": conv_ln_b[l].reshape(1, CONV_W),
            "w_conv_out": w_conv_out[l].astype(BF16),
            "decay_logit": ret_decay_logit[l],
            "w_ret_out": w_ret_out[l].astype(BF16),
            "w_mix_out": w_mix_out[l].astype(BF16),
            "norm2_g": norm2_g[l].reshape(1, D_MODEL),
        }
        wi_bf = w_ffn_in[l].astype(BF16)
        wo_bf = w_ffn_out[l].astype(BF16)
        last = l == depth - 1

        x1p, h2p, (sf, sb) = _trunk_layer(xp, bp, tp, mod3[0:1], bp * tp, None, None, True, w)
        x1s, h2s, _ = _trunk_layer(xs, bs, ts, mod3[1:n_cond], ts, rope,
                                   (state_ret_fwd[:, l], state_ret_bwd[:, l]), False, w)
        new_f.append(sf)
        new_b.append(sb)
        xp = _ffn_call(h2p, x1p, mod3[0:1], bp * tp, final_g, last, wi_bf, wo_bf)
        xs = _ffn_call(h2s, x1s, mod3[1:n_cond], ts, final_g, last, wi_bf, wo_bf)

    y_prompt = xp.reshape(bp, tp, D_MODEL)
    y_sample = xs.reshape(bs, ts, D_MODEL)
    return (y_prompt, y_sample, jnp.stack(new_f, axis=1), jnp.stack(new_b, axis=1))
```
